```python
import jax, jax.numpy as jnp
from jax import lax
import numpy as np

D_MODEL = 2048
BATCH = 8
SEQ = 4096
DEPTH = 4

CHUNK = 64
D_MIX = D_MODEL
CONV_CH = D_MIX // 2
N_HEADS = 8
HEAD_DIM = (D_MIX - CONV_CH) // N_HEADS
ATTN_W = N_HEADS * HEAD_DIM
CONV_K = 31
FFN_CONV_K = 3
D_FF = 5632
Q_BLOCK = 128
EPS = 1e-6
IN_COLS = 2 * CONV_CH + 3 * ATTN_W + N_HEADS

kernel_name = "hymba_conformer_fox_convffn_sandwich"


def rms_norm(x, g):
    xf = x.astype(jnp.float32)
    y = xf * lax.rsqrt(jnp.mean(xf * xf, axis=-1, keepdims=True) + EPS)
    return (y * g.astype(jnp.float32)).astype(x.dtype)


def layer_norm(x, g, b):
    xf = x.astype(jnp.float32)
    mu = jnp.mean(xf, axis=-1, keepdims=True)
    xc = xf - mu
    y = xc * lax.rsqrt(jnp.mean(xc * xc, axis=-1, keepdims=True) + EPS)
    return (y * g.astype(jnp.float32) + b.astype(jnp.float32)).astype(x.dtype)


def causal_depthwise_conv(x, w):
    K, C = w.shape
    return lax.conv_general_dilated(
        x, w[:, None, :].astype(x.dtype), window_strides=(1,), padding=[(K - 1, 0)],
        dimension_numbers=("NWC", "WIO", "NWC"), feature_group_count=C)


def conformer_conv(u, w_dw, b_dw, ln_g, ln_b):
    a, g = jnp.split(u, 2, axis=-1)
    h = a * jax.nn.sigmoid(g)
    h = causal_depthwise_conv(h, w_dw) + b_dw
    h = layer_norm(h, ln_g, ln_b)
    return jax.nn.silu(h)


def forgetting_attention(q, k, v, f_logit):
    B, S = q.shape[0], q.shape[1]
    q = q.reshape(B, S, N_HEADS, HEAD_DIM)
    k = k.reshape(B, S, N_HEADS, HEAD_DIM)
    v = v.reshape(B, S, N_HEADS, HEAD_DIM)
    log_f = jax.nn.log_sigmoid(f_logit.astype(jnp.float32))
    c = jnp.cumsum(log_f, axis=1).transpose(0, 2, 1)
    nb = S // Q_BLOCK
    qb = q.reshape(B, nb, Q_BLOCK, N_HEADS, HEAD_DIM).transpose(1, 0, 2, 3, 4)
    cqb = c.reshape(B, N_HEADS, nb, Q_BLOCK).transpose(2, 0, 1, 3)
    k_pos = jnp.arange(S)
    scale = HEAD_DIM ** -0.5

    def one_block(args):
        qi, cqi, i = args
        s = jnp.einsum("bqhd,bkhd->bhqk", qi, k, preferred_element_type=jnp.float32) * scale
        s = s + cqi[:, :, :, None] - c[:, :, None, :]
        q_pos = i * Q_BLOCK + jnp.arange(Q_BLOCK)
        mask = k_pos[None, :] <= q_pos[:, None]
        s = jnp.where(mask[None, None], s, -jnp.inf)
        p = jax.nn.softmax(s, axis=-1)
        return jnp.einsum("bhqk,bkhd->bqhd", p.astype(v.dtype), v)

    o = lax.map(one_block, (qb, cqb, jnp.arange(nb, dtype=jnp.int32)))
    return o.transpose(1, 0, 2, 3, 4).reshape(B, S, ATTN_W)


def setup_inputs(seed: int = 0) -> dict:
    key = jax.random.key(seed)
    ks = jax.random.split(key, 16)
    f32 = jnp.float32

    def nrm(k, shape, scale):
        return jax.random.normal(k, shape, f32) * scale

    def gain(k, shape):
        return 1.0 + 0.05 * jax.random.normal(k, shape, f32)

    return {
        "x": jax.random.normal(ks[0], (BATCH, SEQ, D_MODEL), f32),
        "pre_mix_g": gain(ks[1], (DEPTH, D_MODEL)),
        "w_in": nrm(ks[2], (DEPTH, D_MODEL, IN_COLS), D_MODEL ** -0.5),
        "b_forget": jax.random.uniform(ks[3], (DEPTH, N_HEADS), f32, 1.0, 6.0),
        "conv_w": nrm(ks[4], (DEPTH, CONV_K, CONV_CH), CONV_K ** -0.5),
        "conv_b": nrm(ks[5], (DEPTH, CONV_CH), 0.02),
        "conv_ln_g": gain(ks[6], (DEPTH, CONV_CH)),
        "conv_ln_b": nrm(ks[7], (DEPTH, CONV_CH), 0.02),
        "w_out": nrm(ks[8], (DEPTH, D_MIX, D_MODEL), D_MIX ** -0.5),
        "post_mix_g": gain(ks[9], (DEPTH, D_MODEL)),
        "pre_ffn_g": gain(ks[10], (DEPTH, D_MODEL)),
        "w_up": nrm(ks[11], (DEPTH, D_MODEL, 2 * D_FF), D_MODEL ** -0.5),
        "ffn_conv_w": nrm(ks[12], (DEPTH, FFN_CONV_K, 2 * D_FF), FFN_CONV_K ** -0.5),
        "w_down": nrm(ks[13], (DEPTH, D_FF, D_MODEL), D_FF ** -0.5),
        "post_ffn_g": gain(ks[14], (DEPTH, D_MODEL)),
    }


def reference(x, pre_mix_g, w_in, b_forget, conv_w, conv_b, conv_ln_g, conv_ln_b,
              w_out, post_mix_g, pre_ffn_g, w_up, ffn_conv_w, w_down, post_ffn_g):
    c0 = 2 * CONV_CH
    for l in range(DEPTH):
        h = rms_norm(x, pre_mix_g[l])
        u = jnp.einsum("bsd,de->bse", h, w_in[l])
        u_conv = u[..., :c0]
        q = u[..., c0:c0 + ATTN_W]
        k = u[..., c0 + ATTN_W:c0 + 2 * ATTN_W]
        v = u[..., c0 + 2 * ATTN_W:c0 + 3 * ATTN_W]
        f_logit = u[..., c0 + 3 * ATTN_W:] + b_forget[l]
        a = conformer_conv(u_conv, conv_w[l], conv_b[l], conv_ln_g[l], conv_ln_b[l])
        b = forgetting_attention(q, k, v, f_logit)
        mix = jnp.einsum("bse,ed->bsd", jnp.concatenate([a, b], axis=-1), w_out[l])
        x = x + rms_norm(mix, post_mix_g[l])
        h = rms_norm(x, pre_ffn_g[l])
        z = causal_depthwise_conv(jnp.einsum("bsd,df->bsf", h, w_up[l]), ffn_conv_w[l])
        zg, zv = jnp.split(z, 2, axis=-1)
        y = jnp.einsum("bsf,fd->bsd", jax.nn.gelu(zg, approximate=True) * zv, w_down[l])
        x = x + rms_norm(y, post_ffn_g[l])
    return x
```

```python
import functools

import jax
import jax.numpy as jnp
from jax import lax
from jax.experimental import pallas as pl
from jax.experimental.pallas import tpu as pltpu

F32 = jnp.float32
BF16 = jnp.bfloat16
EPS = 1e-6
LOG2E = 1.4426950408889634
LANES = 128
SUBLANES = 8
MIB = 1024 * 1024
VMEM_LIMIT = 56 * MIB


def _tile(n, pref):
    t = min(n, pref)
    while n % t:
        t //= 2
    return t


def _params(*sem):
    return pltpu.CompilerParams(dimension_semantics=sem, vmem_limit_bytes=VMEM_LIMIT)


def _rms(x, g):
    return x * lax.rsqrt(jnp.mean(x * x, axis=-1, keepdims=True) + EPS) * g


def _rms_kernel(x_ref, g_ref, h_ref):
    h_ref[...] = _rms(x_ref[...], g_ref[...]).astype(h_ref.dtype)


def _rms_call(x, g):
    t, d = x.shape
    tm = _tile(t, 512)
    return pl.pallas_call(
        _rms_kernel,
        grid=(t // tm,),
        in_specs=[pl.BlockSpec((tm, d), lambda i: (i, 0)), pl.BlockSpec((1, d), lambda i: (0, 0))],
        out_specs=pl.BlockSpec((tm, d), lambda i: (i, 0)),
        out_shape=jax.ShapeDtypeStruct((t, d), BF16),
        compiler_params=_params("parallel"),
        name="rms_in",
    )(x, g)


def _glu_kernel(h_ref, wa_ref, wg_ref, o_ref):
    h = h_ref[...]
    a = jnp.dot(h, wa_ref[...], preferred_element_type=F32)
    g = jnp.dot(h, wg_ref[...], preferred_element_type=F32)
    o_ref[...] = a * jax.nn.sigmoid(g)


def _glu_call(h, w_glu, c):
    t, d = h.shape
    tm, tn = _tile(t, 1024), _tile(c, 512)
    nb = c // tn
    return pl.pallas_call(
        _glu_kernel,
        grid=(t // tm, nb),
        in_specs=[pl.BlockSpec((tm, d), lambda i, j: (i, 0)),
                  pl.BlockSpec((d, tn), lambda i, j: (0, j)),
                  pl.BlockSpec((d, tn), lambda i, j: (0, j + nb))],
        out_specs=pl.BlockSpec((tm, tn), lambda i, j: (i, j)),
        out_shape=jax.ShapeDtypeStruct((t, c), F32),
        compiler_params=_params("parallel", "parallel"),
        name="in_glu",
    )(h, w_glu, w_glu)


def _mm_kernel(h_ref, w_ref, o_ref):
    o_ref[...] = jnp.dot(h_ref[...], w_ref[...], preferred_element_type=F32).astype(o_ref.dtype)


def _matmul_call(h, w):
    t, d = h.shape
    n = w.shape[1]
    tm, tn = _tile(t, 1024), _tile(n, 512)
    return pl.pallas_call(
        _mm_kernel,
        grid=(t // tm, n // tn),
        in_specs=[pl.BlockSpec((tm, d), lambda i, j: (i, 0)), pl.BlockSpec((d, tn), lambda i, j: (0, j))],
        out_specs=pl.BlockSpec((tm, tn), lambda i, j: (i, j)),
        out_shape=jax.ShapeDtypeStruct((t, n), BF16),
        compiler_params=_params("parallel", "parallel"),
        name="in_qkv",
    )(h, w)


def _fgate_kernel(h_ref, wf_ref, bf_ref, c_ref, carry_ref, *, ts):
    @pl.when(pl.program_id(1) == 0)
    def _():
        carry_ref[...] = jnp.zeros_like(carry_ref)

    f = jnp.dot(h_ref[...], wf_ref[...], preferred_element_type=F32) + bf_ref[...]
    lf = jnp.minimum(f, 0.0) - jnp.log1p(jnp.exp(-jnp.abs(f)))
    rows = lax.broadcasted_iota(jnp.int32, lf.shape, 0)
    s = 1
    while s < ts:
        lf = lf + jnp.where(rows >= s, pltpu.roll(lf, s, axis=0), 0.0)
        s *= 2
    c = lf + carry_ref[0:1, :]
    c_ref[...] = c * LOG2E
    carry_ref[...] = jnp.broadcast_to(c[ts - 1:ts, :], carry_ref.shape)


def _fgate_call(h, wf, bf, batch, seq):
    t, d = h.shape
    ts = _tile(seq, 1024)
    nt = seq // ts
    return pl.pallas_call(
        functools.partial(_fgate_kernel, ts=ts),
        grid=(batch, nt),
        in_specs=[pl.BlockSpec((ts, d), lambda b, i: (b * nt + i, 0)),
                  pl.BlockSpec((d, LANES), lambda b, i: (0, 0)),
                  pl.BlockSpec((1, LANES), lambda b, i: (0, 0))],
        out_specs=pl.BlockSpec((ts, LANES), lambda b, i: (b * nt + i, 0)),
        out_shape=jax.ShapeDtypeStruct((t, LANES), F32),
        scratch_shapes=[pltpu.VMEM((SUBLANES, LANES), F32)],
        compiler_params=_params("parallel", "arbitrary"),
        name="fgate",
    )(h, wf, bf)


def _conv_kernel(x_ref, halo_ref, w_ref, b_ref, g_ref, beta_ref, o_ref, xs_ref, ys_ref, *, ts, kw, halo, rc, cc):
    i = pl.program_id(1)
    xs_ref[0, 0:halo, :] = jnp.where(i == 0, 0.0, halo_ref[0])
    xs_ref[0, halo:, :] = x_ref[0]
    n_sh = ts + halo - SUBLANES
    for r in range(1, SUBLANES):
        xs_ref[r, 0:n_sh, :] = xs_ref[0, r:r + n_sh, :]
    c = x_ref.shape[-1]
    base = halo - (kw - 1)

    def row_chunk(r, carry):
        r0 = pl.multiple_of(r * rc, rc)
        for c0 in range(0, c, cc):
            acc = jnp.zeros((rc, cc), F32)
            for k in range(kw):
                a8, sh = divmod(base + k, SUBLANES)
                win = xs_ref[sh, pl.ds(r0 + a8 * SUBLANES, rc), c0:c0 + cc]
                acc = acc + win * w_ref[k:k + 1, c0:c0 + cc]
            ys_ref[pl.ds(r0, rc), c0:c0 + cc] = acc
        return carry

    lax.fori_loop(0, ts // rc, row_chunk, 0)
    y = ys_ref[...] + b_ref[...]
    mu = jnp.mean(y, axis=-1, keepdims=True)
    yc = y - mu
    z = yc * lax.rsqrt(jnp.mean(yc * yc, axis=-1, keepdims=True) + EPS) * g_ref[...] + beta_ref[...]
    o_ref[...] = (z * jax.nn.sigmoid(z)).astype(o_ref.dtype)


def _conv_call(glu, w, b, g, beta, batch, seq):
    c = glu.shape[-1]
    kw = w.shape[0]
    halo = 32
    assert kw - 1 <= halo
    ts = _tile(seq, 256)
    nt = seq // ts
    hb = ts // halo
    wp = jnp.zeros((halo, c), F32).at[:kw].set(w)
    kern = functools.partial(_conv_kernel, ts=ts, kw=kw, halo=halo, rc=32, cc=_tile(c, 256))
    return pl.pallas_call(
        kern,
        grid=(batch, nt),
        in_specs=[pl.BlockSpec((1, ts, c), lambda bi, i: (bi, i, 0)),
                  pl.BlockSpec((1, halo, c), lambda bi, i: (bi, jnp.maximum(i * hb - 1, 0), 0)),
                  pl.BlockSpec((halo, c), lambda bi, i: (0, 0)),
                  pl.BlockSpec((1, c), lambda bi, i: (0, 0)),
                  pl.BlockSpec((1, c), lambda bi, i: (0, 0)),
                  pl.BlockSpec((1, c), lambda bi, i: (0, 0))],
        out_specs=pl.BlockSpec((ts, c), lambda bi, i: (bi * nt + i, 0)),
        out_shape=jax.ShapeDtypeStruct((batch * seq, c), BF16),
        scratch_shapes=[pltpu.VMEM((SUBLANES, ts + halo, c), F32), pltpu.VMEM((ts, c), F32)],
        compiler_params=_params("parallel", "parallel"),
        name="conf_conv",
    )(glu, glu, wp, b, g, beta)


def _attn_kernel(q_ref, k_ref, v_ref, cq_ref, ck_ref, o_ref, m_ref, l_ref, acc_ref, *, ta, scale2):
    h = pl.program_id(1)
    i = pl.program_id(2)
    q = q_ref[...]
    lane = lax.broadcasted_iota(jnp.int32, cq_ref.shape, 1)
    cq = jnp.sum(jnp.where(lane == h, cq_ref[...], 0.0), axis=-1, keepdims=True)
    m_ref[...] = jnp.full_like(m_ref, -jnp.inf)
    l_ref[...] = jnp.zeros_like(l_ref)
    acc_ref[...] = jnp.zeros_like(acc_ref)

    def step(j, masked):
        off = pl.multiple_of(j * ta, ta)
        k = k_ref[pl.ds(off, ta), :]
        v = v_ref[pl.ds(off, ta), :]
        ck = ck_ref[0, j]
        s = lax.dot_general(q, k, (((1,), (1,)), ((), ())), preferred_element_type=F32)
        s = s * scale2 + (cq - ck)
        if masked:
            row = lax.broadcasted_iota(jnp.int32, s.shape, 0)
            col = lax.broadcasted_iota(jnp.int32, s.shape, 1)
            s = jnp.where(col <= row, s, -jnp.inf)
        m_prev = m_ref[...]
        m_new = jnp.maximum(m_prev, jnp.max(s, axis=-1, keepdims=True))
        alpha = jnp.exp2(m_prev - m_new)
        p = jnp.exp2(s - m_new)
        l_ref[...] = alpha * l_ref[...] + jnp.sum(p, axis=-1, keepdims=True)
        acc_ref[...] = alpha * acc_ref[...] + jnp.dot(p.astype(v.dtype), v, preferred_element_type=F32)
        m_ref[...] = m_new

    def body(j, carry):
        step(j, False)
        return carry

    lax.fori_loop(0, i, body, 0)
    step(i, True)
    o_ref[...] = (acc_ref[...] / l_ref[...]).astype(o_ref.dtype)


def _attn_call(qkv, c2, ck, batch, seq, n_heads, hd):
    t = qkv.shape[0]
    ta = ck.shape[-1]
    nq = seq // ta
    kern = functools.partial(_attn_kernel, ta=ta, scale2=float(hd) ** -0.5 * LOG2E)
    return pl.pallas_call(
        kern,
        grid=(batch, n_heads, nq),
        in_specs=[pl.BlockSpec((ta, hd), lambda b, h, i: (b * nq + i, h)),
                  pl.BlockSpec((seq, hd), lambda b, h, i: (b, n_heads + h)),
                  pl.BlockSpec((seq, hd), lambda b, h, i: (b, 2 * n_heads + h)),
                  pl.BlockSpec((ta, LANES), lambda b, h, i: (b * nq + i, 0)),
                  pl.BlockSpec((1, nq, 1, ta), lambda b, h, i: (b * n_heads + h, 0, 0, 0))],
        out_specs=pl.BlockSpec((ta, hd), lambda b, h, i: (b * nq + i, h)),
        out_shape=jax.ShapeDtypeStruct((t, n_heads * hd), BF16),
        scratch_shapes=[pltpu.VMEM((ta, 1), F32), pltpu.VMEM((ta, 1), F32), pltpu.VMEM((ta, hd), F32)],
        compiler_params=_params("parallel", "parallel", "arbitrary"),
        name="fox_attn",
    )(qkv, qkv, qkv, c2, ck)


def _out_kernel(a_ref, b_ref, wa_ref, wb_ref, x_ref, g1_ref, g2_ref, x1_ref, h2_ref):
    mix = jnp.dot(a_ref[...], wa_ref[...], preferred_element_type=F32)
    mix = mix + jnp.dot(b_ref[...], wb_ref[...], preferred_element_type=F32)
    x1 = x_ref[...] + _rms(mix, g1_ref[...])
    x1_ref[...] = x1
    h2_ref[...] = _rms(x1, g2_ref[...]).astype(h2_ref.dtype)


def _out_call(a, b, w_out, x, g1, g2):
    t, d = x.shape
    ca, cb = a.shape[1], b.shape[1]
    assert ca == cb
    tm = _tile(t, 512)
    row = lambda i: (i, 0)
    return pl.pallas_call(
        _out_kernel,
        grid=(t // tm,),
        in_specs=[pl.BlockSpec((tm, ca), row), pl.BlockSpec((tm, cb), row),
                  pl.BlockSpec((ca, d), lambda i: (0, 0)), pl.BlockSpec((cb, d), lambda i: (1, 0)),
                  pl.BlockSpec((tm, d), row),
                  pl.BlockSpec((1, d), lambda i: (0, 0)), pl.BlockSpec((1, d), lambda i: (0, 0))],
        out_specs=[pl.BlockSpec((tm, d), row), pl.BlockSpec((tm, d), row)],
        out_shape=[jax.ShapeDtypeStruct((t, d), F32), jax.ShapeDtypeStruct((t, d), BF16)],
        compiler_params=_params("parallel"),
        name="out_proj",
    )(a, b, w_out, w_out, x, g1, g2)


def _conv3(u, prev, w):
    w0, w1, w2 = w[0:1, :], w[1:2, :], w[2:3, :]
    z = w2 * u + w1 * pltpu.roll(u, 1, axis=0) + w0 * pltpu.roll(u, 2, axis=0)
    top = jnp.concatenate([prev, u[0:SUBLANES, :]], axis=0)
    t1 = pltpu.roll(top, 1, axis=0)[SUBLANES:, :]
    t2 = pltpu.roll(top, 2, axis=0)[SUBLANES:, :]
    z_top = w2 * u[0:SUBLANES, :] + w1 * t1 + w0 * t2
    return jnp.concatenate([z_top, z[SUBLANES:, :]], axis=0)


def _ffn_kernel(h_ref, wg_ref, wv_ref, cwg_ref, cwv_ref, wd_ref, x_ref, gp_ref, gn_ref,
                x2_ref, hn_ref, cg_ref, cv_ref, *, tiles_per_seq):
    i, j = pl.program_id(0), pl.program_id(1)
    h = h_ref[...]
    tm = h.shape[0]
    ug = jnp.dot(h, wg_ref[...], preferred_element_type=F32)
    uv = jnp.dot(h, wv_ref[...], preferred_element_type=F32)
    first = (i % tiles_per_seq) == 0
    pg = jnp.where(first, 0.0, cg_ref[j])
    pv = jnp.where(first, 0.0, cv_ref[j])
    cg_ref[j] = ug[tm - SUBLANES:, :]
    cv_ref[j] = uv[tm - SUBLANES:, :]
    zg = _conv3(ug, pg, cwg_ref[...])
    zv = _conv3(uv, pv, cwv_ref[...])
    act = (jax.nn.gelu(zg, approximate=True) * zv).astype(wd_ref.dtype)
    y = jnp.dot(act, wd_ref[...], preferred_element_type=F32)

    @pl.when(j == 0)
    def _():
        x2_ref[...] = y

    @pl.when(j > 0)
    def _():
        x2_ref[...] += y

    @pl.when(j == pl.num_programs(1) - 1)
    def _():
        x2 = x_ref[...] + _rms(x2_ref[...], gp_ref[...])
        x2_ref[...] = x2
        hn_ref[...] = _rms(x2, gn_ref[...]).astype(hn_ref.dtype)


def _ffn_call(h, w_up, cw, w_down, x, g_post, g_next, seq):
    t, d = x.shape
    f = w_down.shape[0]
    tm, tf = _tile(min(t, seq), 512), _tile(f, 512)
    nf = f // tf
    row = lambda i, j: (i, 0)
    const = lambda i, j: (0, 0)
    kern = functools.partial(_ffn_kernel, tiles_per_seq=seq // tm)
    return pl.pallas_call(
        kern,
        grid=(t // tm, nf),
        in_specs=[pl.BlockSpec((tm, d), row),
                  pl.BlockSpec((d, tf), lambda i, j: (0, j)), pl.BlockSpec((d, tf), lambda i, j: (0, j + nf)),
                  pl.BlockSpec((3, tf), lambda i, j: (0, j)), pl.BlockSpec((3, tf), lambda i, j: (0, j + nf)),
                  pl.BlockSpec((tf, d), lambda i, j: (j, 0)),
                  pl.BlockSpec((tm, d), row), pl.BlockSpec((1, d), const), pl.BlockSpec((1, d), const)],
        out_specs=[pl.BlockSpec((tm, d), row), pl.BlockSpec((tm, d), row)],
        out_shape=[jax.ShapeDtypeStruct((t, d), F32), jax.ShapeDtypeStruct((t, d), BF16)],
        scratch_shapes=[pltpu.VMEM((nf, SUBLANES, tf), F32), pltpu.VMEM((nf, SUBLANES, tf), F32)],
        compiler_params=_params("arbitrary", "arbitrary"),
        name="conv_ffn",
    )(h, w_up, w_up, cw, cw, w_down, x, g_post, g_next)


def kernel(x, pre_mix_g, w_in, b_forget, conv_w, conv_b, conv_ln_g, conv_ln_b, w_out, post_mix_g, pre_ffn_g,
           w_up, ffn_conv_w, w_down, post_ffn_g):
    batch, seq, d = x.shape
    depth = w_in.shape[0]
    c = conv_w.shape[-1]
    n_heads = b_forget.shape[-1]
    aw = w_out.shape[1] - c
    hd = aw // n_heads
    assert hd == LANES and n_heads <= LANES
    t = batch * seq
    ta = _tile(seq, 512)
    nq = seq // ta

    xf = x.reshape(t, d)
    h = _rms_call(xf, pre_mix_g[0][None, :])
    for l in range(depth):
        w_l = w_in[l].astype(BF16)
        w_glu, w_qkv = w_l[:, :2 * c], w_l[:, 2 * c:2 * c + 3 * aw]
        wf = jnp.zeros((d, LANES), BF16).at[:, :n_heads].set(w_l[:, 2 * c + 3 * aw:])
        bf = jnp.zeros((1, LANES), F32).at[0, :n_heads].set(b_forget[l])

        glu = _glu_call(h, w_glu, c)
        a = _conv_call(glu.reshape(batch, seq, c), conv_w[l], conv_b[l][None, :], conv_ln_g[l][None, :],
                       conv_ln_b[l][None, :], batch, seq)
        qkv = _matmul_call(h, w_qkv)
        c2 = _fgate_call(h, wf, bf, batch, seq)
        ck = c2.reshape(batch, seq, LANES)[:, :, :n_heads].transpose(0, 2, 1).reshape(batch * n_heads, nq, 1, ta)
        b = _attn_call(qkv, c2, ck, batch, seq, n_heads, hd)
        x1, h2 = _out_call(a, b, w_out[l].astype(BF16), xf, post_mix_g[l][None, :], pre_ffn_g[l][None, :])
        g_next = pre_mix_g[(l + 1) % depth][None, :]
        xf, h = _ffn_call(h2, w_up[l].astype(BF16), ffn_conv_w[l], w_down[l].astype(BF16), x1,
                          post_ffn_g[l][None, :], g_next, seq)
    return xf.reshape(batch, seq, d)
```

```python
import functools

import jax
import jax.numpy as jnp
from jax import lax
from jax.experimental import pallas as pl
from jax.experimental.pallas import tpu as pltpu

F32 = jnp.float32
BF16 = jnp.bfloat16
EPS = 1e-6
LOG2E = 1.4426950408889634
LANES = 128
SUBLANES = 8
MIB = 1024 * 1024
VMEM_LIMIT = 56 * MIB


def _tile(n, pref):
    t = min(n, pref)
    while n % t:
        t //= 2
    return t


def _params(*sem):
    return pltpu.CompilerParams(dimension_semantics=sem, vmem_limit_bytes=VMEM_LIMIT)


def _rms(x, g):
    return x * lax.rsqrt(jnp.mean(x * x, axis=-1, keepdims=True) + EPS) * g


def _rms_kernel(x_ref, g_ref, h_ref):
    h_ref[...] = _rms(x_ref[...], g_ref[...]).astype(h_ref.dtype)


def _rms_call(x, g):
    t, d = x.shape
    tm = _tile(t, 512)
    return pl.pallas_call(
        _rms_kernel,
        grid=(t // tm,),
        in_specs=[pl.BlockSpec((tm, d), lambda i: (i, 0)), pl.BlockSpec((1, d), lambda i: (0, 0))],
        out_specs=pl.BlockSpec((tm, d), lambda i: (i, 0)),
        out_shape=jax.ShapeDtypeStruct((t, d), BF16),
        compiler_params=_params("parallel"),
        name="rms_in",
    )(x, g)


def _glu_kernel(h_ref, wa_ref, wg_ref, o_ref):
    h = h_ref[...]
    a = jnp.dot(h, wa_ref[...], preferred_element_type=F32)
    g = jnp.dot(h, wg_ref[...], preferred_element_type=F32)
    o_ref[...] = a * jax.nn.sigmoid(g)


def _glu_call(h, w_glu, c):
    t, d = h.shape
    tm, tn = _tile(t, 1024), _tile(c, 512)
    nb = c // tn
    return pl.pallas_call(
        _glu_kernel,
        grid=(t // tm, nb),
        in_specs=[pl.BlockSpec((tm, d), lambda i, j: (i, 0)),
                  pl.BlockSpec((d, tn), lambda i, j: (0, j)),
                  pl.BlockSpec((d, tn), lambda i, j: (0, j + nb))],
        out_specs=pl.BlockSpec((tm, tn), lambda i, j: (i, j)),
        out_shape=jax.ShapeDtypeStruct((t, c), F32),
        compiler_params=_params("parallel", "parallel"),
        name="in_glu",
    )(h, w_glu, w_glu)


def _mm_kernel(h_ref, w_ref, o_ref, *, q_tiles, q_scale):
    y = jnp.dot(h_ref[...], w_ref[...], preferred_element_type=F32)
    y = y * jnp.where(pl.program_id(1) < q_tiles, q_scale, 1.0)
    o_ref[...] = y.astype(o_ref.dtype)


def _matmul_call(h, w, q_cols, q_scale):
    t, d = h.shape
    n = w.shape[1]
    tm, tn = _tile(t, 1024), _tile(q_cols, 512)
    assert n % tn == 0
    return pl.pallas_call(
        functools.partial(_mm_kernel, q_tiles=q_cols // tn, q_scale=q_scale),
        grid=(t // tm, n // tn),
        in_specs=[pl.BlockSpec((tm, d), lambda i, j: (i, 0)), pl.BlockSpec((d, tn), lambda i, j: (0, j))],
        out_specs=pl.BlockSpec((tm, tn), lambda i, j: (i, j)),
        out_shape=jax.ShapeDtypeStruct((t, n), BF16),
        compiler_params=_params("parallel", "parallel"),
        name="in_qkv",
    )(h, w)


def _fgate_kernel(h_ref, wf_ref, bf_ref, c_ref, carry_ref, *, ts):
    @pl.when(pl.program_id(1) == 0)
    def _():
        carry_ref[...] = jnp.zeros_like(carry_ref)

    f = jnp.dot(h_ref[...], wf_ref[...], preferred_element_type=F32) + bf_ref[...]
    lf = jnp.minimum(f, 0.0) - jnp.log1p(jnp.exp(-jnp.abs(f)))
    rows = lax.broadcasted_iota(jnp.int32, lf.shape, 0)
    s = 1
    while s < ts:
        lf = lf + jnp.where(rows >= s, pltpu.roll(lf, s, axis=0), 0.0)
        s *= 2
    c = lf + carry_ref[0:1, :]
    c_ref[...] = c * LOG2E
    carry_ref[...] = jnp.broadcast_to(c[ts - 1:ts, :], carry_ref.shape)


def _fgate_call(h, wf, bf, batch, seq):
    t, d = h.shape
    ts = _tile(seq, 1024)
    nt = seq // ts
    return pl.pallas_call(
        functools.partial(_fgate_kernel, ts=ts),
        grid=(batch, nt),
        in_specs=[pl.BlockSpec((ts, d), lambda b, i: (b * nt + i, 0)),
                  pl.BlockSpec((d, LANES), lambda b, i: (0, 0)),
                  pl.BlockSpec((1, LANES), lambda b, i: (0, 0))],
        out_specs=pl.BlockSpec((ts, LANES), lambda b, i: (b * nt + i, 0)),
        out_shape=jax.ShapeDtypeStruct((t, LANES), F32),
        scratch_shapes=[pltpu.VMEM((SUBLANES, LANES), F32)],
        compiler_params=_params("parallel", "arbitrary"),
        name="fgate",
    )(h, wf, bf)


def _conv_kernel(x_ref, halo_ref, w_ref, b_ref, g_ref, beta_ref, o_ref, xs_ref, ys_ref, *, ts, kw, halo, rc, cc):
    i = pl.program_id(1)
    xs_ref[0, 0:halo, :] = jnp.where(i == 0, 0.0, halo_ref[0])
    xs_ref[0, halo:, :] = x_ref[0]
    n_sh = ts + halo - SUBLANES
    for r in range(1, SUBLANES):
        xs_ref[r, 0:n_sh, :] = xs_ref[0, r:r + n_sh, :]
    c = x_ref.shape[-1]
    base = halo - (kw - 1)

    def row_chunk(r, carry):
        r0 = pl.multiple_of(r * rc, rc)
        for c0 in range(0, c, cc):
            acc = jnp.zeros((rc, cc), F32)
            for k in range(kw):
                a8, sh = divmod(base + k, SUBLANES)
                win = xs_ref[sh, pl.ds(r0 + a8 * SUBLANES, rc), c0:c0 + cc]
                acc = acc + win * w_ref[k:k + 1, c0:c0 + cc]
            ys_ref[pl.ds(r0, rc), c0:c0 + cc] = acc
        return carry

    lax.fori_loop(0, ts // rc, row_chunk, 0)
    y = ys_ref[...] + b_ref[...]
    mu = jnp.mean(y, axis=-1, keepdims=True)
    yc = y - mu
    z = yc * lax.rsqrt(jnp.mean(yc * yc, axis=-1, keepdims=True) + EPS) * g_ref[...] + beta_ref[...]
    o_ref[...] = (z * jax.nn.sigmoid(z)).astype(o_ref.dtype)


def _conv_call(glu, w, b, g, beta, batch, seq):
    c = glu.shape[-1]
    kw = w.shape[0]
    halo = 32
    assert kw - 1 <= halo
    ts = _tile(seq, 256)
    nt = seq // ts
    hb = ts // halo
    wp = jnp.zeros((halo, c), F32).at[:kw].set(w)
    kern = functools.partial(_conv_kernel, ts=ts, kw=kw, halo=halo, rc=32, cc=_tile(c, 256))
    return pl.pallas_call(
        kern,
        grid=(batch, nt),
        in_specs=[pl.BlockSpec((1, ts, c), lambda bi, i: (bi, i, 0)),
                  pl.BlockSpec((1, halo, c), lambda bi, i: (bi, jnp.maximum(i * hb - 1, 0), 0)),
                  pl.BlockSpec((halo, c), lambda bi, i: (0, 0)),
                  pl.BlockSpec((1, c), lambda bi, i: (0, 0)),
                  pl.BlockSpec((1, c), lambda bi, i: (0, 0)),
                  pl.BlockSpec((1, c), lambda bi, i: (0, 0))],
        out_specs=pl.BlockSpec((ts, c), lambda bi, i: (bi * nt + i, 0)),
        out_shape=jax.ShapeDtypeStruct((batch * seq, c), BF16),
        scratch_shapes=[pltpu.VMEM((SUBLANES, ts + halo, c), F32), pltpu.VMEM((ts, c), F32)],
        compiler_params=_params("parallel", "parallel"),
        name="conf_conv",
    )(glu, glu, wp, b, g, beta)


def _split3(x):
    hi = x.astype(BF16).astype(F32)
    r = x - hi
    mid = r.astype(BF16).astype(F32)
    lo = (r - mid).astype(BF16).astype(F32)
    return hi, mid, lo


def _attn_kernel(q_ref, k_ref, v_ref, c_ref, o_ref, ka_ref, va_ref, qx_ref, s_ref, p_ref, m_ref, al_ref, acc_ref,
                 *, ta, rg, cs):
    g = pl.program_id(1)
    i = pl.program_id(2)
    hd = LANES
    seq = k_ref.shape[0]
    n_grp = k_ref.shape[1] // hd

    @pl.when(i == 0)
    def _():
        def build(r, carry):
            rows = pl.ds(pl.multiple_of(r * cs, cs), cs)
            lane = lax.broadcasted_iota(jnp.int32, (cs, LANES), 1)
            for e in range(n_grp):
                head = g * n_grp + e
                c = jnp.sum(jnp.where(lane == head, c_ref[rows, :], 0.0), axis=-1, keepdims=True)
                hi, mid, lo = _split3(c)
                kx = jnp.where(lane == 0, -hi, jnp.where(lane == 1, -mid, jnp.where(lane == 2, -lo,
                                                                                     jnp.where(lane < 6, 1.0, 0.0))))
                qx = jnp.where(lane < 3, 1.0, jnp.where(lane == 3, hi, jnp.where(lane == 4, mid,
                                                                                 jnp.where(lane == 5, lo, 0.0))))
                ka_ref[e, rows, 0:hd] = k_ref[rows, e * hd:(e + 1) * hd]
                ka_ref[e, rows, hd:] = kx.astype(ka_ref.dtype)
                qx_ref[e, rows, :] = qx.astype(qx_ref.dtype)
                va_ref[e, rows, 0:hd] = v_ref[rows, e * hd:(e + 1) * hd]
                va_ref[e, rows, hd:] = jnp.ones((cs, LANES), va_ref.dtype)
            return carry

        lax.fori_loop(0, seq // cs, build, 0)

    q_rows = pl.ds(pl.multiple_of(i * ta, ta), ta)
    qa = [jnp.concatenate([q_ref[:, e * hd:(e + 1) * hd], qx_ref[e, q_rows, :]], axis=1) for e in range(n_grp)]
    m_ref[...] = jnp.full_like(m_ref, -jnp.inf)
    acc_ref[...] = jnp.zeros_like(acc_ref)

    nc = ta // LANES

    def scores(j, e):
        off = pl.multiple_of(j * ta, ta)
        s_ref[e] = lax.dot_general(qa[e], ka_ref[e, pl.ds(off, ta), :], (((1,), (1,)), ((), ())),
                                   preferred_element_type=F32)

    def head_step(j, diag, e):
        off = pl.multiple_of(j * ta, ta)
        for r in range(0, ta, rg):
            vis = [0 if diag and c * LANES > r + rg - 1 else (1 if diag and c * LANES + LANES - 1 > r else 2)
                   for c in range(nc)]
            sc = []
            for c in range(nc):
                if vis[c] == 0:
                    sc.append(None)
                    continue
                x = s_ref[e, r:r + rg, c * LANES:(c + 1) * LANES]
                if vis[c] == 1:
                    row = r + lax.broadcasted_iota(jnp.int32, x.shape, 0)
                    col = c * LANES + lax.broadcasted_iota(jnp.int32, x.shape, 1)
                    x = jnp.where(col <= row, x, -jnp.inf)
                sc.append(x)
            mc = functools.reduce(jnp.maximum, [x for x in sc if x is not None])
            m_prev = m_ref[e, r:r + rg, :]
            m_new = jnp.maximum(m_prev, jnp.max(mc, axis=-1, keepdims=True))
            al_ref[e, r:r + rg, :] = jnp.exp2(m_prev - m_new)
            m_ref[e, r:r + rg, :] = m_new
            for c in range(nc):
                pc = jnp.zeros((rg, LANES), F32) if sc[c] is None else jnp.exp2(sc[c] - m_new)
                p_ref[e, r:r + rg, c * LANES:(c + 1) * LANES] = pc.astype(p_ref.dtype)
        pv = jnp.dot(p_ref[e], va_ref[e, pl.ds(off, ta), :], preferred_element_type=F32)
        al = al_ref[e]
        acc_ref[e, :, 0:hd] = al * acc_ref[e, :, 0:hd] + pv[:, 0:hd]
        acc_ref[e, :, hd:] = al * acc_ref[e, :, hd:] + pv[:, hd:]

    for e in range(n_grp):
        scores(0, e)

    def body(j, carry):
        for e in range(n_grp):
            head_step(j, False, e)
            scores(j + 1, e)
        return carry

    lax.fori_loop(0, i, body, 0)
    for e in range(n_grp):
        head_step(i, True, e)
    for e in range(n_grp):
        o_ref[:, e * hd:(e + 1) * hd] = (acc_ref[e, :, 0:hd] / acc_ref[e, :, hd:]).astype(o_ref.dtype)


def _attn_call(qkv, c2, batch, seq, n_heads, hd):
    t = qkv.shape[0]
    ta = _tile(seq, 512)
    nq = seq // ta
    n_grp = 2 if n_heads % 2 == 0 else 1
    ng = n_heads // n_grp
    gw = n_grp * hd
    kern = functools.partial(_attn_kernel, ta=ta, rg=_tile(ta, 64), cs=_tile(seq, 512))
    return pl.pallas_call(
        kern,
        grid=(batch, ng, nq),
        in_specs=[pl.BlockSpec((ta, gw), lambda b, g, i: (b * nq + i, g)),
                  pl.BlockSpec((seq, gw), lambda b, g, i: (b, ng + g)),
                  pl.BlockSpec((seq, gw), lambda b, g, i: (b, 2 * ng + g)),
                  pl.BlockSpec((seq, LANES), lambda b, g, i: (b, 0))],
        out_specs=pl.BlockSpec((ta, gw), lambda b, g, i: (b * nq + i, g)),
        out_shape=jax.ShapeDtypeStruct((t, n_heads * hd), BF16),
        scratch_shapes=[pltpu.VMEM((n_grp, seq, 2 * hd), BF16), pltpu.VMEM((n_grp, seq, 2 * hd), BF16),
                        pltpu.VMEM((n_grp, seq, LANES), BF16),
                        pltpu.VMEM((n_grp, ta, ta), F32), pltpu.VMEM((n_grp, ta, ta), BF16),
                        pltpu.VMEM((n_grp, ta, LANES), F32), pltpu.VMEM((n_grp, ta, LANES), F32),
                        pltpu.VMEM((n_grp, ta, 2 * hd), F32)],
        compiler_params=_params("parallel", "arbitrary", "arbitrary"),
        name="fox_attn",
    )(qkv, qkv, qkv, c2)


def _out_kernel(a_ref, b_ref, wa_ref, wb_ref, x_ref, g1_ref, g2_ref, x1_ref, h2_ref):
    mix = jnp.dot(a_ref[...], wa_ref[...], preferred_element_type=F32)
    mix = mix + jnp.dot(b_ref[...], wb_ref[...], preferred_element_type=F32)
    x1 = x_ref[...] + _rms(mix, g1_ref[...])
    x1_ref[...] = x1
    h2_ref[...] = _rms(x1, g2_ref[...]).astype(h2_ref.dtype)


def _out_call(a, b, w_out, x, g1, g2):
    t, d = x.shape
    ca, cb = a.shape[1], b.shape[1]
    assert ca == cb
    tm = _tile(t, 512)
    row = lambda i: (i, 0)
    return pl.pallas_call(
        _out_kernel,
        grid=(t // tm,),
        in_specs=[pl.BlockSpec((tm, ca), row), pl.BlockSpec((tm, cb), row),
                  pl.BlockSpec((ca, d), lambda i: (0, 0)), pl.BlockSpec((cb, d), lambda i: (1, 0)),
                  pl.BlockSpec((tm, d), row),
                  pl.BlockSpec((1, d), lambda i: (0, 0)), pl.BlockSpec((1, d), lambda i: (0, 0))],
        out_specs=[pl.BlockSpec((tm, d), row), pl.BlockSpec((tm, d), row)],
        out_shape=[jax.ShapeDtypeStruct((t, d), F32), jax.ShapeDtypeStruct((t, d), BF16)],
        compiler_params=_params("parallel"),
        name="out_proj",
    )(a, b, w_out, w_out, x, g1, g2)


def _conv3(u, prev, w):
    w0, w1, w2 = w[0:1, :], w[1:2, :], w[2:3, :]
    z = w2 * u + w1 * pltpu.roll(u, 1, axis=0) + w0 * pltpu.roll(u, 2, axis=0)
    top = jnp.concatenate([prev, u[0:SUBLANES, :]], axis=0)
    t1 = pltpu.roll(top, 1, axis=0)[SUBLANES:, :]
    t2 = pltpu.roll(top, 2, axis=0)[SUBLANES:, :]
    z_top = w2 * u[0:SUBLANES, :] + w1 * t1 + w0 * t2
    return jnp.concatenate([z_top, z[SUBLANES:, :]], axis=0)


def _ffn_kernel(h_ref, wg_ref, wv_ref, cwg_ref, cwv_ref, wd_ref, x_ref, gp_ref, gn_ref,
                x2_ref, hn_ref, cg_ref, cv_ref, *, tiles_per_seq):
    i, j = pl.program_id(0), pl.program_id(1)

    @pl.when(j == 0)
    def _():
        x2_ref[...] = jnp.zeros_like(x2_ref)

    h = h_ref[...]
    tm = h.shape[0]
    ug = jnp.dot(h, wg_ref[...], preferred_element_type=F32)
    uv = jnp.dot(h, wv_ref[...], preferred_element_type=F32)
    first = (i % tiles_per_seq) == 0
    pg = jnp.where(first, 0.0, cg_ref[j])
    pv = jnp.where(first, 0.0, cv_ref[j])
    cg_ref[j] = ug[tm - SUBLANES:, :]
    cv_ref[j] = uv[tm - SUBLANES:, :]
    zg = _conv3(ug, pg, cwg_ref[...])
    zv = _conv3(uv, pv, cwv_ref[...])
    act = (jax.nn.gelu(zg, approximate=True) * zv).astype(wd_ref.dtype)
    x2_ref[...] += jnp.dot(act, wd_ref[...], preferred_element_type=F32)

    @pl.when(j == pl.num_programs(1) - 1)
    def _():
        x2 = x_ref[...] + _rms(x2_ref[...], gp_ref[...])
        x2_ref[...] = x2
        hn_ref[...] = _rms(x2, gn_ref[...]).astype(hn_ref.dtype)


def _ffn_call(h, w_up, cw, w_down, x, g_post, g_next, seq):
    t, d = x.shape
    f = w_down.shape[0]
    tm, tf = _tile(min(t, seq), 512), _tile(f, 512)
    nf = f // tf
    row = lambda i, j: (i, 0)
    const = lambda i, j: (0, 0)
    kern = functools.partial(_ffn_kernel, tiles_per_seq=seq // tm)
    return pl.pallas_call(
        kern,
        grid=(t // tm, nf),
        in_specs=[pl.BlockSpec((tm, d), row),
                  pl.BlockSpec((d, tf), lambda i, j: (0, j)), pl.BlockSpec((d, tf), lambda i, j: (0, j + nf)),
                  pl.BlockSpec((3, tf), lambda i, j: (0, j)), pl.BlockSpec((3, tf), lambda i, j: (0, j + nf)),
                  pl.BlockSpec((tf, d), lambda i, j: (j, 0)),
                  pl.BlockSpec((tm, d), row), pl.BlockSpec((1, d), const), pl.BlockSpec((1, d), const)],
        out_specs=[pl.BlockSpec((tm, d), row), pl.BlockSpec((tm, d), row)],
        out_shape=[jax.ShapeDtypeStruct((t, d), F32), jax.ShapeDtypeStruct((t, d), BF16)],
        scratch_shapes=[pltpu.VMEM((nf, SUBLANES, tf), F32), pltpu.VMEM((nf, SUBLANES, tf), F32)],
        compiler_params=_params("arbitrary", "arbitrary"),
        name="conv_ffn",
    )(h, w_up, w_up, cw, cw, w_down, x, g_post, g_next)


def kernel(x, pre_mix_g, w_in, b_forget, conv_w, conv_b, conv_ln_g, conv_ln_b, w_out, post_mix_g, pre_ffn_g,
           w_up, ffn_conv_w, w_down, post_ffn_g):
    batch, seq, d = x.shape
    depth = w_in.shape[0]
    c = conv_w.shape[-1]
    n_heads = b_forget.shape[-1]
    aw = w_out.shape[1] - c
    hd = aw // n_heads
    assert hd == LANES and n_heads <= LANES
    t = batch * seq
    q_scale = float(hd) ** -0.5 * LOG2E

    xf = x.reshape(t, d)
    h = _rms_call(xf, pre_mix_g[0][None, :])
    for l in range(depth):
        w_l = w_in[l].astype(BF16)
        w_glu, w_qkv = w_l[:, :2 * c], w_l[:, 2 * c:2 * c + 3 * aw]
        wf = jnp.zeros((d, LANES), BF16).at[:, :n_heads].set(w_l[:, 2 * c + 3 * aw:])
        bf = jnp.zeros((1, LANES), F32).at[0, :n_heads].set(b_forget[l])

        glu = _glu_call(h, w_glu, c)
        a = _conv_call(glu.reshape(batch, seq, c), conv_w[l], conv_b[l][None, :], conv_ln_g[l][None, :],
                       conv_ln_b[l][None, :], batch, seq)
        qkv = _matmul_call(h, w_qkv, aw, q_scale)
        c2 = _fgate_call(h, wf, bf, batch, seq)
        b = _attn_call(qkv, c2, batch, seq, n_heads, hd)
        x1, h2 = _out_call(a, b, w_out[l].astype(BF16), xf, post_mix_g[l][None, :], pre_ffn_g[l][None, :])
        g_next = pre_mix_g[(l + 1) % depth][None, :]
        xf, h = _ffn_call(h2, w_up[l].astype(BF16), ffn_conv_w[l], w_down[l].astype(BF16), x1,
                          post_ffn_g[l][None, :], g_next, seq)
    return xf.reshape(batch, seq, d)
```
